```python
import jax, jax.numpy as jnp
from jax import lax
import numpy as np

D_MODEL = 1024
BATCH = 16
SEQ = 2048
DEPTH = 1

GRID_W = 64
CTX_LEN = 256
EPS = 1e-6
CONV_WIDTH = D_MODEL
CONV_K = 31
GLA_HEADS = 4
GLA_DK = D_MODEL // 2
GLA_DV = D_MODEL
HEAD_K = GLA_DK // GLA_HEADS
HEAD_V = GLA_DV // GLA_HEADS
GATE_RANK = 16
GATE_TAU = 16.0
CHUNK = 64
IN_SIZES = (CONV_WIDTH, CONV_WIDTH, CONV_WIDTH, GLA_DK, GLA_DK, GLA_DV, GATE_RANK, GATE_RANK, GLA_DV, D_MODEL, D_MODEL)
N_IN = 3 * CONV_WIDTH + 2 * GLA_DK + 2 * GLA_DV + 2 * GATE_RANK + 2 * D_MODEL
STATE_SIZES = (GLA_DK, GLA_DV, GATE_RANK, GATE_RANK)
STATE_LO = 3 * CONV_WIDTH + GLA_DK
STATE_HI = STATE_LO + GLA_DK + GLA_DV + 2 * GATE_RANK

kernel_name = "hybrid_conformer_gla_dit_block"


def split_cols(p, sizes):
    idx = [int(i) for i in np.cumsum(sizes)[:-1]]
    return jnp.split(p, idx, axis=-1)


def rmsnorm(x, g):
    xf = x.astype(jnp.float32)
    y = xf * lax.rsqrt(jnp.mean(xf * xf, axis=-1, keepdims=True) + EPS)
    return y * g.astype(jnp.float32)


def layernorm(x, g, b):
    xf = x.astype(jnp.float32)
    mu = jnp.mean(xf, axis=-1, keepdims=True)
    var = jnp.mean(jnp.square(xf - mu), axis=-1, keepdims=True)
    return (xf - mu) * lax.rsqrt(var + EPS) * g + b


def ada_mod(cvec, w, b):
    m = jax.nn.silu(cvec) @ w + b
    return jnp.split(m, 3, axis=-1)


def dwconv(x, w, b):
    C = x.shape[-1]
    y = lax.conv_general_dilated(x, w.astype(x.dtype)[:, None, :], (1,), [(CONV_K // 2, CONV_K // 2)],
                                 dimension_numbers=('NWC', 'WIO', 'NWC'), feature_group_count=C)
    return y + b.astype(x.dtype)


def axial_dwconv(a, w, b, rows):
    Bn, S, C = a.shape
    half = C // 2
    ah = a[..., :half].reshape(Bn * rows, GRID_W, half)
    yh = dwconv(ah, w[:, :half], b[:half]).reshape(Bn, S, half)
    av = a[..., half:].reshape(Bn, rows, GRID_W, C - half).transpose(0, 2, 1, 3).reshape(Bn * GRID_W, rows, C - half)
    yv = dwconv(av, w[:, half:], b[half:]).reshape(Bn, GRID_W, rows, C - half).transpose(0, 2, 1, 3).reshape(Bn, S, C - half)
    return jnp.concatenate([yh, yv], axis=-1)


def conv_branch(glu_v, glu_g, z, conv_fn, ln_g, ln_b, proj):
    a = glu_v * jax.nn.sigmoid(glu_g)
    a = conv_fn(a)
    a = jax.nn.silu(layernorm(a, ln_g, ln_b))
    return (a * jax.nn.silu(z)) @ proj


def heads(t, hd):
    Bn, T, _ = t.shape
    return t.reshape(Bn, T, GLA_HEADS, hd).transpose(0, 2, 1, 3).astype(jnp.float32)


def log_decay(lr, up, bias):
    return heads(jax.nn.log_sigmoid((lr @ up + bias).astype(jnp.float32)) / GATE_TAU, HEAD_K)


def flip(t):
    return jnp.flip(t, axis=2)


def gla_final_state(k, v, g):
    b = jnp.cumsum(g, axis=2)
    w = jnp.exp(b[:, :, -1:, :] - b)
    return jnp.einsum('bhtd,bhte->bhde', k * w, v)


def gla_chunk_scan(q, k, v, g, s0):
    Bn, H, T, _ = q.shape
    dv = v.shape[-1]
    n = T // CHUNK

    def chunks(t):
        return jnp.moveaxis(t.reshape(Bn, H, n, CHUNK, t.shape[-1]), 2, 0)

    lower = jnp.tril(jnp.ones((CHUNK, CHUNK), dtype=bool))

    def step(S, inp):
        qc, kc, vc, gc = inp
        b = jnp.cumsum(gc, axis=2)
        o_inter = jnp.einsum('bhld,bhde->bhle', qc * jnp.exp(b), S)
        rel = jnp.where(lower[:, :, None], b[:, :, :, None, :] - b[:, :, None, :, :], -jnp.inf)
        A = jnp.einsum('bhid,bhjd,bhijd->bhij', qc, kc, jnp.exp(rel))
        o_intra = jnp.einsum('bhij,bhje->bhie', A, vc)
        b_last = b[:, :, -1:, :]
        S_new = jnp.exp(b_last[:, :, 0, :, None]) * S + jnp.einsum('bhld,bhle->bhde', kc * jnp.exp(b_last - b), vc)
        return S_new, o_inter + o_intra

    S_fin, o = lax.scan(step, s0, (chunks(q), chunks(k), chunks(v), chunks(g)))
    return jnp.moveaxis(o, 0, 2).reshape(Bn, H, T, dv), S_fin


def bidir_gla(q, k, v, gf, gb, s_f, s_b):
    o_f, _ = gla_chunk_scan(q, k, v, gf, s_f)
    o_b, _ = gla_chunk_scan(flip(q), flip(k), flip(v), flip(gb), s_b)
    return o_f + flip(o_b)


def gla_output(o, r, norm_g, proj):
    o = o * lax.rsqrt(jnp.mean(o * o, axis=-1, keepdims=True) + EPS) * norm_g
    Bn, H, T, dv = o.shape
    o = o.transpose(0, 2, 1, 3).reshape(Bn, T, H * dv)
    return (o * jax.nn.silu(r)) @ proj


def setup_inputs(seed: int = 0) -> dict:
    key = jax.random.key(seed)
    ks = jax.random.split(key, 24)

    def nrm(k, shape, scale=1.0):
        return jax.random.normal(k, shape, jnp.float32) * scale

    return {
        "x": nrm(ks[0], (BATCH, SEQ, D_MODEL)),
        "c": nrm(ks[1], (BATCH, D_MODEL)),
        "ctx": nrm(ks[2], (BATCH, CTX_LEN, D_MODEL)),
        "c_ctx": nrm(ks[3], (D_MODEL,)),
        "ada_w": nrm(ks[4], (DEPTH, D_MODEL, 3 * D_MODEL), D_MODEL ** -0.5),
        "ada_b": nrm(ks[5], (DEPTH, 3 * D_MODEL), 0.02),
        "norm_g": 1.0 + nrm(ks[6], (DEPTH, D_MODEL), 0.02),
        "w_in": nrm(ks[7], (DEPTH, D_MODEL, N_IN), D_MODEL ** -0.5),
        "b_in": nrm(ks[8], (DEPTH, N_IN), 0.02),
        "conv_w": nrm(ks[9], (DEPTH, CONV_K, CONV_WIDTH), CONV_K ** -0.5),
        "conv_b": nrm(ks[10], (DEPTH, CONV_WIDTH), 0.02),
        "conv_ln_g": 1.0 + nrm(ks[11], (DEPTH, CONV_WIDTH), 0.02),
        "conv_ln_b": nrm(ks[12], (DEPTH, CONV_WIDTH), 0.02),
        "conv_proj": nrm(ks[13], (DEPTH, CONV_WIDTH, D_MODEL), CONV_WIDTH ** -0.5),
        "decay_up_fwd": nrm(ks[14], (DEPTH, GATE_RANK, GLA_DK), GATE_RANK ** -0.5),
        "decay_bias_fwd": nrm(ks[15], (DEPTH, GLA_DK), 0.1),
        "decay_up_bwd": nrm(ks[16], (DEPTH, GATE_RANK, GLA_DK), GATE_RANK ** -0.5),
        "decay_bias_bwd": nrm(ks[17], (DEPTH, GLA_DK), 0.1),
        "gla_norm_g": 1.0 + nrm(ks[18], (DEPTH, HEAD_V), 0.02),
        "gla_proj": nrm(ks[19], (DEPTH, GLA_DV, D_MODEL), GLA_DV ** -0.5),
        "w_out": nrm(ks[20], (DEPTH, D_MODEL, D_MODEL), D_MODEL ** -0.5),
        "final_norm_g": 1.0 + nrm(ks[21], (D_MODEL,), 0.02),
    }


def reference(x, c, ctx, c_ctx, ada_w, ada_b, norm_g, w_in, b_in, conv_w, conv_b, conv_ln_g, conv_ln_b,
              conv_proj, decay_up_fwd, decay_bias_fwd, decay_up_bwd, decay_bias_bwd, gla_norm_g, gla_proj,
              w_out, final_norm_g):
    Bn, S, _ = x.shape
    rows = S // GRID_W
    h = x
    hc = ctx
    for l in range(DEPTH):
        last = l == DEPTH - 1
        shift, scale, gate = ada_mod(c, ada_w[l], ada_b[l])
        shift_c, scale_c, gate_c = ada_mod(c_ctx, ada_w[l], ada_b[l])
        u = rmsnorm(h, norm_g[l]) * (1.0 + scale[:, None, :]) + shift[:, None, :]
        uc = rmsnorm(hc, norm_g[l]) * (1.0 + scale_c) + shift_c

        if last:
            pc = uc @ w_in[l][:, STATE_LO:STATE_HI] + b_in[l][STATE_LO:STATE_HI]
            kc_, vc_, afc, abc = split_cols(pc, STATE_SIZES)
        else:
            (gvc, ggc, zc, qc_, kc_, vc_, afc, abc, rc, mgc_conv, mgc_gla) = split_cols(uc @ w_in[l] + b_in[l], IN_SIZES)
        k_ctx = heads(kc_, HEAD_K)
        v_ctx = heads(vc_, HEAD_V)
        gf_ctx = log_decay(afc, decay_up_fwd[l], decay_bias_fwd[l])
        gb_ctx = log_decay(abc, decay_up_bwd[l], decay_bias_bwd[l])
        s_f = gla_final_state(k_ctx, v_ctx, gf_ctx)
        s_b = gla_final_state(flip(k_ctx), flip(v_ctx), flip(gb_ctx))

        (gv, gg, z, q_, k_, v_, af, ab, r, mg_conv, mg_gla) = split_cols(u @ w_in[l] + b_in[l], IN_SIZES)
        y_conv = conv_branch(gv, gg, z, lambda a: axial_dwconv(a, conv_w[l], conv_b[l], rows),
                             conv_ln_g[l], conv_ln_b[l], conv_proj[l])
        q = heads(q_, HEAD_K) * (HEAD_K ** -0.5)
        k = heads(k_, HEAD_K)
        v = heads(v_, HEAD_V)
        gf = log_decay(af, decay_up_fwd[l], decay_bias_fwd[l])
        gb = log_decay(ab, decay_up_bwd[l], decay_bias_bwd[l])
        o = bidir_gla(q, k, v, gf, gb, s_f, s_b)
        y_gla = gla_output(o, r, gla_norm_g[l], gla_proj[l])
        merged = jax.nn.sigmoid(mg_conv) * y_conv + jax.nn.sigmoid(mg_gla) * y_gla
        h_new = h + gate[:, None, :] * (merged @ w_out[l])

        if not last:
            yc_conv = conv_branch(gvc, ggc, zc, lambda a: dwconv(a, conv_w[l], conv_b[l]),
                                  conv_ln_g[l], conv_ln_b[l], conv_proj[l])
            q_ctx = heads(qc_, HEAD_K) * (HEAD_K ** -0.5)
            zero_state = jnp.zeros((Bn, GLA_HEADS, HEAD_K, HEAD_V), jnp.float32)
            oc = bidir_gla(q_ctx, k_ctx, v_ctx, gf_ctx, gb_ctx, zero_state, zero_state)
            yc_gla = gla_output(oc, rc, gla_norm_g[l], gla_proj[l])
            merged_c = jax.nn.sigmoid(mgc_conv) * yc_conv + jax.nn.sigmoid(mgc_gla) * yc_gla
            hc = hc + gate_c * (merged_c @ w_out[l])
        h = h_new
    return rmsnorm(h, final_norm_g)
```

```python
import functools

import jax
import jax.numpy as jnp
from jax import lax
from jax.experimental import pallas as pl
from jax.experimental.pallas import tpu as pltpu

F32 = jnp.float32
BF16 = jnp.bfloat16

GRID_W = 64
CONV_K = 31
CONV_HALF = CONV_K // 2
GLA_HEADS = 4
GATE_RANK = 16
GATE_TAU = 16.0
CHUNK = 64
EPS = 1e-6
LANES = 128
SUBLANES = 8
VMEM_LIMIT = 60 * 1024 * 1024


def _sigmoid(x):
    return 0.5 * jnp.tanh(0.5 * x) + 0.5


def _silu(x):
    return x * _sigmoid(x)


def _log_sigmoid(x):
    return jnp.minimum(x, 0.0) - jnp.log(1.0 + jnp.exp(-jnp.abs(x)))


def _split_bf16(x):
    hi = x.astype(BF16)
    lo = (x - hi.astype(F32)).astype(BF16)
    return hi, lo


def _dot(a, b):
    return jnp.dot(a, b, preferred_element_type=F32)


def _dot_nt(a, b):
    return lax.dot_general(a, b, (((1,), (1,)), ((), ())), preferred_element_type=F32)


def _dot_tn(a, b):
    return lax.dot_general(a, b, (((0,), (0,)), ((), ())), preferred_element_type=F32)


def _params(*sem):
    return pltpu.CompilerParams(dimension_semantics=sem, vmem_limit_bytes=VMEM_LIMIT)


def _ada_kernel(c_ref, w_ref, b_ref, o_ref):
    a_hi, a_lo = _split_bf16(_silu(c_ref[...]))
    w_hi, w_lo = _split_bf16(w_ref[...])
    o_ref[...] = _dot(a_hi, w_hi) + _dot(a_lo, w_hi) + _dot(a_hi, w_lo) + b_ref[...]


def _ada_call(cs, w, b):
    rows, d = cs.shape
    n = w.shape[1]
    bn = 512
    return pl.pallas_call(
        _ada_kernel,
        grid=(n // bn,),
        in_specs=[pl.BlockSpec((rows, d), lambda j: (0, 0)),
                  pl.BlockSpec((d, bn), lambda j: (0, j)),
                  pl.BlockSpec((1, bn), lambda j: (0, j))],
        out_specs=pl.BlockSpec((rows, bn), lambda j: (0, j)),
        out_shape=jax.ShapeDtypeStruct((rows, n), F32),
        compiler_params=_params("parallel"),
        name="ada",
    )(cs, w, b)


def _modulated_norm(x, ng, scale, shift):
    ms = jnp.mean(x * x, axis=-1, keepdims=True)
    return x * lax.rsqrt(ms + EPS) * (ng * (1.0 + scale)) + shift


def _tri(n, upper):
    r = lax.broadcasted_iota(jnp.int32, (n, n), 0)
    c = lax.broadcasted_iota(jnp.int32, (n, n), 1)
    return (c >= r) if upper else (c <= r)


def _cumsum_rows(tri_bf16, g):
    hi, lo = _split_bf16(g)
    return _dot(tri_bf16, hi) + _dot(tri_bf16, lo)


def _ctx_kernel(x_ref, shift_ref, scale_ref, ng_ref, wk_ref, wv_ref, wd_ref, bk_ref, bv_ref, bd_ref,
                up_ref, dbias_ref, sf_ref, sb_ref, *, dk, hk, hv):
    t = x_ref.shape[0]
    u = _modulated_norm(x_ref[...], ng_ref[...], scale_ref[...], shift_ref[...]).astype(BF16)
    k = _dot(u, wk_ref[...]) + bk_ref[...]
    v = (_dot(u, wv_ref[...]) + bv_ref[...]).astype(BF16)
    dec = _dot(u, wd_ref[...]) + bd_ref[...]
    g = _log_sigmoid(_dot(dec.astype(BF16), up_ref[...]) + dbias_ref[...]) * (1.0 / GATE_TAU)
    incl = _tri(t, upper=False)
    lower = incl.astype(BF16)
    strict = jnp.logical_and(incl, jnp.logical_not(_tri(t, upper=True))).astype(BF16)
    bf = _cumsum_rows(lower, g[:, :dk])
    wf = jnp.exp(bf[t - 1:t, :] - bf)
    wb = jnp.exp(_cumsum_rows(strict, g[:, dk:]))
    kf = (k * wf).astype(BF16)
    kb = (k * wb).astype(BF16)
    for h in range(GLA_HEADS):
        vh = v[:, h * hv:(h + 1) * hv]
        sf_ref[h] = _dot_tn(vh, kf[:, h * hk:(h + 1) * hk])
        sb_ref[h] = _dot_tn(vh, kb[:, h * hk:(h + 1) * hk])


def _ctx_call(ctx, shift_c, scale_c, ng, w_perm, b_perm, up, dbias, offs):
    bsz, t, d = ctx.shape
    dk, dv = offs["dk"], offs["dv"]
    hk, hv = dk // GLA_HEADS, dv // GLA_HEADS
    kblk, vblk, dblk = offs["k"] // dk, offs["v"] // dv, offs["dec"] // LANES
    row = lambda b: (0, 0)
    kern = functools.partial(_ctx_kernel, dk=dk, hk=hk, hv=hv)
    state = jax.ShapeDtypeStruct((bsz, GLA_HEADS, hv, hk), F32)
    return pl.pallas_call(
        kern,
        grid=(bsz,),
        in_specs=[pl.BlockSpec((None, t, d), lambda b: (b, 0, 0)),
                  pl.BlockSpec((1, d), row), pl.BlockSpec((1, d), row), pl.BlockSpec((1, d), row),
                  pl.BlockSpec((d, dk), lambda b: (0, kblk)),
                  pl.BlockSpec((d, dv), lambda b: (0, vblk)),
                  pl.BlockSpec((d, LANES), lambda b: (0, dblk)),
                  pl.BlockSpec((1, dk), lambda b: (0, kblk)),
                  pl.BlockSpec((1, dv), lambda b: (0, vblk)),
                  pl.BlockSpec((1, LANES), lambda b: (0, dblk)),
                  pl.BlockSpec((LANES, 2 * dk), row),
                  pl.BlockSpec((1, 2 * dk), row)],
        out_specs=[pl.BlockSpec((None, GLA_HEADS, hv, hk), lambda b: (b, 0, 0, 0))] * 2,
        out_shape=[state, state],
        compiler_params=_params("parallel"),
        name="ctx_states",
    )(ctx, shift_c, scale_c, ng, w_perm, w_perm, w_perm, b_perm, b_perm, b_perm, up, dbias)


def _inproj_kernel(x_ref, shift_ref, scale_ref, ng_ref, w_ref, b_ref,
                   a_ref, zs_ref, q_ref, k_ref, v_ref, rs_ref, sgc_ref, sgg_ref, dec_ref, *, offs, q_scale):
    u = _modulated_norm(x_ref[...], ng_ref[...], scale_ref[...], shift_ref[...]).astype(BF16)

    def proj(name, width):
        lo = offs[name]
        return _dot(u, w_ref[:, lo:lo + width]) + b_ref[:, lo:lo + width]

    d, dk, dv = offs["d"], offs["dk"], offs["dv"]
    a_ref[...] = (proj("gv", d) * _sigmoid(proj("gg", d))).astype(BF16)
    zs_ref[...] = _silu(proj("z", d)).astype(BF16)
    q_ref[...] = (proj("q", dk) * q_scale).astype(BF16)
    k_ref[...] = proj("k", dk).astype(BF16)
    v_ref[...] = proj("v", dv).astype(BF16)
    rs_ref[...] = _silu(proj("r", dv)).astype(BF16)
    sgc_ref[...] = _sigmoid(proj("mgc", d)).astype(BF16)
    sgg_ref[...] = _sigmoid(proj("mgg", d)).astype(BF16)
    dec_ref[...] = proj("dec", LANES)


def _inproj_call(x2, shift, scale, ng, w_perm, b_perm, offs, seq, tm):
    m, d = x2.shape
    dk, dv = offs["dk"], offs["dv"]
    n = w_perm.shape[1]
    per_b = seq // tm
    tok = lambda width: pl.BlockSpec((tm, width), lambda i: (i, 0))
    mod = pl.BlockSpec((None, 1, d), lambda i: (i // per_b, 0, 0))
    const = lambda shape: pl.BlockSpec(shape, lambda i: (0, 0), pipeline_mode=pl.Buffered(1))
    kern = functools.partial(_inproj_kernel, offs=offs, q_scale=float(dk // GLA_HEADS) ** -0.5)
    sds = lambda width, dt: jax.ShapeDtypeStruct((m, width), dt)
    return pl.pallas_call(
        kern,
        grid=(m // tm,),
        in_specs=[tok(d), mod, mod, const((1, d)), const((d, n)), const((1, n))],
        out_specs=[tok(d), tok(d), tok(dk), tok(dk), tok(dv), tok(dv), tok(d), tok(d), tok(LANES)],
        out_shape=[sds(d, BF16), sds(d, BF16), sds(dk, BF16), sds(dk, BF16), sds(dv, BF16), sds(dv, BF16),
                   sds(d, BF16), sds(d, BF16), sds(LANES, F32)],
        compiler_params=_params("parallel"),
        name="inproj",
    )(x2, shift, scale, ng, w_perm, b_perm)


def _conv_kernel(a_ref, w_ref, b_ref, y_ref, vpad_ref, hpad_ref, *, rows, half_groups):
    j = pl.program_id(1)
    w = w_ref[...]
    bias = jnp.broadcast_to(b_ref[...], (GRID_W, LANES))
    taps = [jnp.broadcast_to(w[k:k + 1, :], (GRID_W, LANES)) for k in range(CONV_K)]
    vpad_rows = CONV_HALF * GRID_W
    hrow = GRID_W + 4 * SUBLANES
    hoff = 2 * SUBLANES

    @pl.when(j >= half_groups)
    def _():
        zeros = jnp.zeros((vpad_rows, LANES), F32)
        vpad_ref[0:vpad_rows, :] = zeros
        vpad_ref[vpad_rows + rows * GRID_W:2 * vpad_rows + rows * GRID_W, :] = zeros
        vpad_ref[vpad_rows:vpad_rows + rows * GRID_W, :] = a_ref[...].astype(F32)

        def body(r, carry):
            base = pl.multiple_of(r * GRID_W, GRID_W)
            acc = bias
            for k in range(CONV_K):
                acc = acc + taps[k] * vpad_ref[pl.ds(base + k * GRID_W, GRID_W), :]
            y_ref[pl.ds(base, GRID_W), :] = acc
            return carry

        lax.fori_loop(0, rows, body, 0)

    @pl.when(j < half_groups)
    def _():
        zeros = jnp.zeros((hoff, LANES), F32)

        def body(r, carry):
            base = pl.multiple_of(r * GRID_W, GRID_W)
            hpad_ref[0:hoff, :] = zeros
            hpad_ref[hoff:hoff + GRID_W, :] = a_ref[pl.ds(base, GRID_W), :].astype(F32)
            hpad_ref[hoff + GRID_W:hrow, :] = zeros
            xrow = hpad_ref[...]
            phases = [xrow] + [pltpu.roll(xrow, hrow - p, axis=0) for p in range(1, SUBLANES)]
            acc = bias
            for k in range(CONV_K):
                start = hoff - CONV_HALF + k
                ph, al = start % SUBLANES, start - start % SUBLANES
                acc = acc + taps[k] * phases[ph][al:al + GRID_W, :]
            y_ref[pl.ds(base, GRID_W), :] = acc
            return carry

        lax.fori_loop(0, rows, body, 0)


def _conv_call(a2, conv_w, conv_b, bsz, seq):
    m, c = a2.shape
    rows = seq // GRID_W
    groups = c // LANES
    kern = functools.partial(_conv_kernel, rows=rows, half_groups=groups // 2)
    return pl.pallas_call(
        kern,
        grid=(bsz, groups),
        in_specs=[pl.BlockSpec((seq, LANES), lambda b, j: (b, j)),
                  pl.BlockSpec((CONV_K, LANES), lambda b, j: (0, j)),
                  pl.BlockSpec((1, LANES), lambda b, j: (0, j))],
        out_specs=pl.BlockSpec((seq, LANES), lambda b, j: (b, j)),
        out_shape=jax.ShapeDtypeStruct((m, c), F32),
        scratch_shapes=[pltpu.VMEM((seq + 2 * CONV_HALF * GRID_W, LANES), F32),
                        pltpu.VMEM((GRID_W + 4 * SUBLANES, LANES), F32)],
        compiler_params=_params("parallel", "parallel"),
        name="axial_conv",
    )(a2, conv_w, conv_b)


def _gla_kernel(q_ref, k_ref, v_ref, dec_ref, sf0_ref, sb0_ref, up_ref, dbias_ref, o_ref, sf_ref, sb_ref,
                *, dk, hk, hv, n_chunks):
    sf_ref[...] = sf0_ref[...]
    sb_ref[...] = sb0_ref[...]
    o_ref[...] = jnp.zeros(o_ref.shape, F32)
    lower_mask = _tri(CHUNK, upper=False)
    upper_mask = _tri(CHUNK, upper=True)
    lower = lower_mask.astype(BF16)
    upper = upper_mask.astype(BF16)
    mid = CHUNK // 2

    def one_direction(c, s_ref, tri, mask, col0, end_row):
        rows = pl.ds(pl.multiple_of(c * CHUNK, CHUNK), CHUNK)
        x = _dot(dec_ref[rows, :].astype(BF16), up_ref[:, col0:col0 + dk]) + dbias_ref[:, col0:col0 + dk]
        g = _log_sigmoid(x) * (1.0 / GATE_TAU)
        b = _cumsum_rows(tri, g)
        b_mid = b[mid:mid + 1, :]
        b_end = b[end_row:end_row + 1, :]
        e_in = jnp.exp(b)
        e_q = jnp.exp(b - b_mid)
        e_k = jnp.exp(b_mid - b)
        e_out = jnp.exp(b_end - b)
        d_end = jnp.exp(b_end)
        q = q_ref[rows, :].astype(F32)
        k = k_ref[rows, :].astype(F32)
        q_in = (q * e_in).astype(BF16)
        q_t = (q * e_q).astype(BF16)
        k_t = (k * e_k).astype(BF16)
        k_out = (k * e_out).astype(BF16)
        for h in range(GLA_HEADS):
            ks = slice(h * hk, (h + 1) * hk)
            vh = v_ref[rows, h * hv:(h + 1) * hv]
            s = s_ref[h]
            att = jnp.where(mask, _dot_nt(q_t[:, ks], k_t[:, ks]), 0.0).astype(BF16)
            o = _dot_nt(q_in[:, ks], s.astype(BF16)) + _dot(att, vh)
            o_ref[rows, h * hv:(h + 1) * hv] += o
            s_ref[h] = s * d_end[:, ks] + _dot_tn(vh, k_out[:, ks])

    def body(i, carry):
        one_direction(i, sf_ref, lower, lower_mask, 0, CHUNK - 1)
        one_direction(n_chunks - 1 - i, sb_ref, upper, upper_mask, dk, 0)
        return carry

    lax.fori_loop(0, n_chunks, body, 0)


def _gla_call(q2, k2, v2, dec2, sf0, sb0, up, dbias, bsz, seq):
    dk, dv = q2.shape[1], v2.shape[1]
    hk, hv = dk // GLA_HEADS, dv // GLA_HEADS
    kern = functools.partial(_gla_kernel, dk=dk, hk=hk, hv=hv, n_chunks=seq // CHUNK)
    tok = lambda width: pl.BlockSpec((seq, width), lambda b: (b, 0))
    state = pl.BlockSpec((None, GLA_HEADS, hv, hk), lambda b: (b, 0, 0, 0))
    const = lambda shape: pl.BlockSpec(shape, lambda b: (0, 0))
    return pl.pallas_call(
        kern,
        grid=(bsz,),
        in_specs=[tok(dk), tok(dk), tok(dv), tok(LANES), state, state, const((LANES, 2 * dk)), const((1, 2 * dk))],
        out_specs=tok(dv),
        out_shape=jax.ShapeDtypeStruct((bsz * seq, dv), F32),
        scratch_shapes=[pltpu.VMEM((GLA_HEADS, hv, hk), F32), pltpu.VMEM((GLA_HEADS, hv, hk), F32)],
        compiler_params=_params("parallel"),
        name="gla_scan",
    )(q2, k2, v2, dec2, sf0, sb0, up, dbias)


def _out_kernel(x_ref, gate_ref, y_ref, zs_ref, sgc_ref, o_ref, rs_ref, sgg_ref, lng_ref, lnb_ref, cproj_ref,
                hng_ref, gproj_ref, wout_ref, fng_ref, out_ref, *, hv):
    y = y_ref[...]
    mu = jnp.mean(y, axis=-1, keepdims=True)
    yc = y - mu
    var = jnp.mean(yc * yc, axis=-1, keepdims=True)
    act = _silu(yc * lax.rsqrt(var + EPS) * lng_ref[...] + lnb_ref[...]) * zs_ref[...].astype(F32)
    y_conv = _dot(act.astype(BF16), cproj_ref[...])

    o = o_ref[...]
    normed = []
    for h in range(GLA_HEADS):
        oh = o[:, h * hv:(h + 1) * hv]
        ms = jnp.mean(oh * oh, axis=-1, keepdims=True)
        normed.append(oh * lax.rsqrt(ms + EPS) * hng_ref[...])
    og = jnp.concatenate(normed, axis=-1) * rs_ref[...].astype(F32)
    y_gla = _dot(og.astype(BF16), gproj_ref[...])

    merged = sgc_ref[...].astype(F32) * y_conv + sgg_ref[...].astype(F32) * y_gla
    h_new = x_ref[...] + gate_ref[...] * _dot(merged.astype(BF16), wout_ref[...])
    ms = jnp.mean(h_new * h_new, axis=-1, keepdims=True)
    out_ref[...] = h_new * lax.rsqrt(ms + EPS) * fng_ref[...]


def _out_call(x2, gate, y2, zs2, sgc2, o2, rs2, sgg2, lng, lnb, cproj, hng, gproj, wout, fng, seq, tm):
    m, d = x2.shape
    dv = o2.shape[1]
    per_b = seq // tm
    tok = lambda width: pl.BlockSpec((tm, width), lambda i: (i, 0))
    const = lambda shape: pl.BlockSpec(shape, lambda i: (0, 0))
    kern = functools.partial(_out_kernel, hv=dv // GLA_HEADS)
    return pl.pallas_call(
        kern,
        grid=(m // tm,),
        in_specs=[tok(d), pl.BlockSpec((None, 1, d), lambda i: (i // per_b, 0, 0)),
                  tok(d), tok(d), tok(d), tok(dv), tok(dv), tok(d),
                  const((1, d)), const((1, d)), const((d, d)),
                  const((1, dv // GLA_HEADS)), const((dv, d)), const((d, d)), const((1, d))],
        out_specs=tok(d),
        out_shape=jax.ShapeDtypeStruct((m, d), F32),
        compiler_params=_params("parallel"),
        name="out_stage",
    )(x2, gate, y2, zs2, sgc2, o2, rs2, sgg2, lng, lnb, cproj, hng, gproj, wout, fng)


def _layer_params(w_in, b_in, up_f, bias_f, up_b, bias_b, d, dk, dv):
    sizes = (d, d, d, dk, dk, dv, GATE_RANK, GATE_RANK, dv, d, d)
    names = ("gv", "gg", "z", "q", "k", "v", "af", "ab", "r", "mgc", "mgg")
    starts, acc = {}, 0
    for nm, sz in zip(names, sizes):
        starts[nm] = (acc, sz)
        acc += sz
    order = ("gv", "gg", "z", "q", "k", "v", "r", "mgc", "mgg", "af", "ab")
    offs, pos = {"d": d, "dk": dk, "dv": dv}, 0
    for nm in order:
        offs[nm] = pos
        pos += starts[nm][1]
    offs["dec"] = offs["af"]
    pad = offs["dec"] + LANES - pos
    take = lambda arr: [arr[..., starts[nm][0]:starts[nm][0] + starts[nm][1]] for nm in order]
    w_perm = jnp.concatenate(take(w_in) + [jnp.zeros((d, pad), w_in.dtype)], axis=-1).astype(BF16)
    b_perm = jnp.concatenate(take(b_in) + [jnp.zeros((pad,), b_in.dtype)], axis=-1)[None, :]
    up = jnp.zeros((LANES, 2 * dk), F32)
    up = up.at[:GATE_RANK, :dk].set(up_f).at[GATE_RANK:2 * GATE_RANK, dk:].set(up_b).astype(BF16)
    dbias = jnp.concatenate([bias_f, bias_b])[None, :]
    return w_perm, b_perm, up, dbias, offs


def kernel(x, c, ctx, c_ctx, ada_w, ada_b, norm_g, w_in, b_in, conv_w, conv_b, conv_ln_g, conv_ln_b, conv_proj,
           decay_up_fwd, decay_bias_fwd, decay_up_bwd, decay_bias_bwd, gla_norm_g, gla_proj, w_out, final_norm_g):
    bsz, seq, d = x.shape
    depth = ada_w.shape[0]
    assert depth == 1, "single-layer block"
    dk = decay_up_fwd.shape[-1]
    dv = gla_proj.shape[1]
    l = 0

    mod_rows = -(-(bsz + 1) // SUBLANES) * SUBLANES
    cs = jnp.concatenate([c, c_ctx[None, :], jnp.zeros((mod_rows - bsz - 1, d), F32)], axis=0)
    mod = _ada_call(cs, ada_w[l], ada_b[l][None, :])
    shift, scale, gate = (mod[:bsz, i * d:(i + 1) * d].reshape(bsz, 1, d) for i in range(3))
    shift_c, scale_c = (mod[bsz:bsz + 1, i * d:(i + 1) * d] for i in range(2))

    w_perm, b_perm, up, dbias, offs = _layer_params(
        w_in[l], b_in[l], decay_up_fwd[l], decay_bias_fwd[l], decay_up_bwd[l], decay_bias_bwd[l], d, dk, dv)
    ng = norm_g[l][None, :]

    sf0, sb0 = _ctx_call(ctx, shift_c, scale_c, ng, w_perm, b_perm, up, dbias, offs)

    x2 = x.reshape(bsz * seq, d)
    a2, zs2, q2, k2, v2, rs2, sgc2, sgg2, dec2 = _inproj_call(x2, shift, scale, ng, w_perm, b_perm, offs, seq, tm=512)
    y2 = _conv_call(a2, conv_w[l], conv_b[l][None, :], bsz, seq)
    o2 = _gla_call(q2, k2, v2, dec2, sf0, sb0, up, dbias, bsz, seq)
    out = _out_call(x2, gate, y2, zs2, sgc2, o2, rs2, sgg2,
                    conv_ln_g[l][None, :], conv_ln_b[l][None, :], conv_proj[l].astype(BF16),
                    gla_norm_g[l][None, :], gla_proj[l].astype(BF16), w_out[l].astype(BF16),
                    final_norm_g[None, :], seq, tm=256)
    return out.reshape(bsz, seq, d)
```

```python
import functools

import jax
import jax.numpy as jnp
from jax import lax
from jax.experimental import pallas as pl
from jax.experimental.pallas import tpu as pltpu

F32 = jnp.float32
BF16 = jnp.bfloat16

GRID_W = 64
CONV_K = 31
CONV_HALF = CONV_K // 2
GLA_HEADS = 4
GATE_RANK = 16
GATE_TAU = 16.0
CHUNK = 64
EPS = 1e-6
LANES = 128
SUBLANES = 8
VMEM_LIMIT = 60 * 1024 * 1024


def _sigmoid(x):
    return 0.5 * jnp.tanh(0.5 * x) + 0.5


def _silu(x):
    return x * _sigmoid(x)


def _log_sigmoid(x):
    return jnp.minimum(x, 0.0) - jnp.log(1.0 + jnp.exp(-jnp.abs(x)))


def _split_bf16(x):
    hi = x.astype(BF16)
    lo = (x - hi.astype(F32)).astype(BF16)
    return hi, lo


def _dot(a, b):
    return jnp.dot(a, b, preferred_element_type=F32)


def _dot_nt(a, b):
    return lax.dot_general(a, b, (((1,), (1,)), ((), ())), preferred_element_type=F32)


def _dot_tn(a, b):
    return lax.dot_general(a, b, (((0,), (0,)), ((), ())), preferred_element_type=F32)


def _params(*sem):
    return pltpu.CompilerParams(dimension_semantics=sem, vmem_limit_bytes=VMEM_LIMIT)


def _ada_kernel(c_ref, w_ref, b_ref, o_ref):
    a_hi, a_lo = _split_bf16(_silu(c_ref[...]))
    w_hi, w_lo = _split_bf16(w_ref[...])
    o_ref[...] = _dot(a_hi, w_hi) + _dot(a_lo, w_hi) + _dot(a_hi, w_lo) + b_ref[...]


def _ada_call(cs, w, b):
    rows, d = cs.shape
    n = w.shape[1]
    bn = 512
    return pl.pallas_call(
        _ada_kernel,
        grid=(n // bn,),
        in_specs=[pl.BlockSpec((rows, d), lambda j: (0, 0)),
                  pl.BlockSpec((d, bn), lambda j: (0, j)),
                  pl.BlockSpec((1, bn), lambda j: (0, j))],
        out_specs=pl.BlockSpec((rows, bn), lambda j: (0, j)),
        out_shape=jax.ShapeDtypeStruct((rows, n), F32),
        compiler_params=_params("parallel"),
        name="ada",
    )(cs, w, b)


def _modulated_norm(x, ng, scale, shift):
    ms = jnp.mean(x * x, axis=-1, keepdims=True)
    return x * lax.rsqrt(ms + EPS) * (ng * (1.0 + scale)) + shift


def _tri(n, upper):
    r = lax.broadcasted_iota(jnp.int32, (n, n), 0)
    c = lax.broadcasted_iota(jnp.int32, (n, n), 1)
    return (c >= r) if upper else (c <= r)


def _cumsum_rows(tri_bf16, g):
    hi, lo = _split_bf16(g)
    return _dot(tri_bf16, hi) + _dot(tri_bf16, lo)


def _cumsum_rows_stacked(tri2_bf16, g):
    hi, lo = _split_bf16(g)
    return _dot(tri2_bf16, jnp.concatenate([hi, lo], axis=0))


def _ctx_kernel(x_ref, shift_ref, scale_ref, ng_ref, wk_ref, wv_ref, wd_ref, bk_ref, bv_ref, bd_ref,
                up_ref, dbias_ref, sf_ref, sb_ref, *, dk, hk, hv):
    t = x_ref.shape[0]
    u = _modulated_norm(x_ref[...], ng_ref[...], scale_ref[...], shift_ref[...]).astype(BF16)
    k = _dot(u, wk_ref[...]) + bk_ref[...]
    v = (_dot(u, wv_ref[...]) + bv_ref[...]).astype(BF16)
    dec = _dot(u, wd_ref[...]) + bd_ref[...]
    g = _log_sigmoid(_dot(dec.astype(BF16), up_ref[...]) + dbias_ref[...]) * (1.0 / GATE_TAU)
    incl = _tri(t, upper=False)
    lower = incl.astype(BF16)
    strict = jnp.logical_and(incl, jnp.logical_not(_tri(t, upper=True))).astype(BF16)
    bf = _cumsum_rows(lower, g[:, :dk])
    wf = jnp.exp(bf[t - 1:t, :] - bf)
    wb = jnp.exp(_cumsum_rows(strict, g[:, dk:]))
    kf = (k * wf).astype(BF16)
    kb = (k * wb).astype(BF16)
    for h in range(GLA_HEADS):
        vh = v[:, h * hv:(h + 1) * hv]
        sf_ref[h] = _dot_tn(vh, kf[:, h * hk:(h + 1) * hk])
        sb_ref[h] = _dot_tn(vh, kb[:, h * hk:(h + 1) * hk])


def _ctx_call(ctx, shift_c, scale_c, ng, w_a, b_a, w_d, b_d, up, dbias, offs):
    bsz, t, d = ctx.shape
    dk, dv = offs["dk"], offs["dv"]
    hk, hv = dk // GLA_HEADS, dv // GLA_HEADS
    kblk, vblk = offs["k"] // dk, offs["v"] // dv
    row = lambda b: (0, 0)
    kern = functools.partial(_ctx_kernel, dk=dk, hk=hk, hv=hv)
    state = jax.ShapeDtypeStruct((bsz, GLA_HEADS, hv, hk), F32)
    return pl.pallas_call(
        kern,
        grid=(bsz,),
        in_specs=[pl.BlockSpec((None, t, d), lambda b: (b, 0, 0)),
                  pl.BlockSpec((1, d), row), pl.BlockSpec((1, d), row), pl.BlockSpec((1, d), row),
                  pl.BlockSpec((d, dk), lambda b: (0, kblk)),
                  pl.BlockSpec((d, dv), lambda b: (0, vblk)),
                  pl.BlockSpec((d, LANES), row),
                  pl.BlockSpec((1, dk), lambda b: (0, kblk)),
                  pl.BlockSpec((1, dv), lambda b: (0, vblk)),
                  pl.BlockSpec((1, LANES), row),
                  pl.BlockSpec((LANES, 2 * dk), row),
                  pl.BlockSpec((1, 2 * dk), row)],
        out_specs=[pl.BlockSpec((None, GLA_HEADS, hv, hk), lambda b: (b, 0, 0, 0))] * 2,
        out_shape=[state, state],
        compiler_params=_params("parallel"),
        name="ctx_states",
    )(ctx, shift_c, scale_c, ng, w_a, w_a, w_d, b_a, b_a, b_d, up, dbias)


def _inproj_kernel(x_ref, shift_ref, scale_ref, ng_ref, wa_ref, ba_ref, wt_ref, bt_ref, wd_ref, bd_ref, up_ref,
                   dbias_ref, a_ref, zs_ref, v_ref, rs_ref, sgc_ref, sgg_ref, qf_ref, kf_ref, qb_ref, kb_ref, cv_ref,
                   *, offs, q_scale):
    u = _modulated_norm(x_ref[...], ng_ref[...], scale_ref[...], shift_ref[...]).astype(BF16)

    def proj(w_ref, b_ref, lo, width):
        return _dot(u, w_ref[:, lo:lo + width]) + b_ref[:, lo:lo + width]

    d, dk, dv = offs["d"], offs["dk"], offs["dv"]

    q = proj(wa_ref, ba_ref, offs["q"], dk) * q_scale
    k = proj(wa_ref, ba_ref, offs["k"], dk)
    dec = _dot(u, wd_ref[...]) + bd_ref[...]
    g = _log_sigmoid(_dot(dec.astype(BF16), up_ref[...]) + dbias_ref[...]) * (1.0 / GATE_TAU)
    lower = _tri(CHUNK, upper=False).astype(BF16)
    upper = _tri(CHUNK, upper=True).astype(BF16)
    lower2 = jnp.concatenate([lower, lower], axis=1)
    upper2 = jnp.concatenate([upper, upper], axis=1)
    mid = CHUNK // 2

    def gla_operands(c):
        rows = slice(c * CHUNK, (c + 1) * CHUNK)
        for tri2, col0, end_row, q_out, k_out, vec0 in ((lower2, 0, CHUNK - 1, qf_ref, kf_ref, 0),
                                                        (upper2, dk, 0, qb_ref, kb_ref, 3)):
            b = _cumsum_rows_stacked(tri2, g[rows, col0:col0 + dk])
            b_mid = b[mid:mid + 1, :]
            b_end = b[end_row:end_row + 1, :]
            q_out[rows, :] = (q[rows, :] * jnp.exp(b - b_mid)).astype(BF16)
            k_out[rows, :] = (k[rows, :] * jnp.exp(b_mid - b)).astype(BF16)
            base = c * SUBLANES + vec0
            cv_ref[base:base + 1, :] = jnp.exp(b_mid)
            cv_ref[base + 1:base + 2, :] = jnp.exp(b_end - b_mid)
            cv_ref[base + 2:base + 3, :] = jnp.exp(b_end)
        cv_ref[c * SUBLANES + 6:(c + 1) * SUBLANES, :] = jnp.zeros((2, dk), F32)

    n_chunks = x_ref.shape[0] // CHUNK
    wide = (
        (a_ref, lambda: proj(wa_ref, ba_ref, offs["gv"], d) * _sigmoid(proj(wa_ref, ba_ref, offs["gg"], d))),
        (zs_ref, lambda: _silu(proj(wa_ref, ba_ref, offs["z"], d))),
        (v_ref, lambda: proj(wa_ref, ba_ref, offs["v"], dv)),
        (rs_ref, lambda: _silu(proj(wt_ref, bt_ref, offs["r"], dv))),
        (sgc_ref, lambda: _sigmoid(proj(wt_ref, bt_ref, offs["mgc"], d))),
        (sgg_ref, lambda: _sigmoid(proj(wt_ref, bt_ref, offs["mgg"], d))),
    )
    done = 0
    for step, (out_ref, value) in enumerate(wide):
        out_ref[...] = value().astype(BF16)
        target = n_chunks * (step + 1) // len(wide)
        for c in range(done, target):
            gla_operands(c)
        done = target


def _inproj_call(x2, shift, scale, ng, w_a, b_a, w_t, b_t, w_d, b_d, up, dbias, offs, seq, tm):
    m, d = x2.shape
    dk, dv = offs["dk"], offs["dv"]
    per_b = seq // tm
    cv_rows = tm // CHUNK * SUBLANES
    tok = lambda width: pl.BlockSpec((tm, width), lambda i: (i, 0))
    mod = pl.BlockSpec((None, 1, d), lambda i: (i // per_b, 0, 0))
    const = lambda arr: pl.BlockSpec(arr.shape, lambda i: (0, 0), pipeline_mode=pl.Buffered(1))
    kern = functools.partial(_inproj_kernel, offs=offs, q_scale=float(dk // GLA_HEADS) ** -0.5)
    sds = lambda width: jax.ShapeDtypeStruct((m, width), BF16)
    return pl.pallas_call(
        kern,
        grid=(m // tm,),
        in_specs=[tok(d), mod, mod] + [const(arr) for arr in (ng, w_a, b_a, w_t, b_t, w_d, b_d, up, dbias)],
        out_specs=[tok(d), tok(d), tok(dv), tok(dv), tok(d), tok(d), tok(dk), tok(dk), tok(dk), tok(dk),
                   pl.BlockSpec((cv_rows, dk), lambda i: (i, 0))],
        out_shape=[sds(d), sds(d), sds(dv), sds(dv), sds(d), sds(d), sds(dk), sds(dk), sds(dk), sds(dk),
                   jax.ShapeDtypeStruct((m // CHUNK * SUBLANES, dk), F32)],
        compiler_params=_params("parallel"),
        name="inproj",
    )(x2, shift, scale, ng, w_a, b_a, w_t, b_t, w_d, b_d, up, dbias)


def _conv_kernel(a_ref, w_ref, b_ref, y_ref, vpad_ref, hpad_ref, *, rows, half_groups):
    j = pl.program_id(1)
    w = w_ref[...]
    bias = jnp.broadcast_to(b_ref[...], (GRID_W, LANES))
    taps = [jnp.broadcast_to(w[k:k + 1, :], (GRID_W, LANES)) for k in range(CONV_K)]
    vpad_rows = CONV_HALF * GRID_W
    hrow = GRID_W + 4 * SUBLANES
    hoff = 2 * SUBLANES

    @pl.when(j >= half_groups)
    def _():
        zeros = jnp.zeros((vpad_rows, LANES), F32)
        vpad_ref[0:vpad_rows, :] = zeros
        vpad_ref[vpad_rows + rows * GRID_W:2 * vpad_rows + rows * GRID_W, :] = zeros
        vpad_ref[vpad_rows:vpad_rows + rows * GRID_W, :] = a_ref[...].astype(F32)

        def body(r, carry):
            base = pl.multiple_of(r * GRID_W, GRID_W)
            acc = bias
            for k in range(CONV_K):
                acc = acc + taps[k] * vpad_ref[pl.ds(base + k * GRID_W, GRID_W), :]
            y_ref[pl.ds(base, GRID_W), :] = acc
            return carry

        lax.fori_loop(0, rows, body, 0)

    @pl.when(j < half_groups)
    def _():
        zeros = jnp.zeros((hoff, LANES), F32)

        def body(r, carry):
            base = pl.multiple_of(r * GRID_W, GRID_W)
            hpad_ref[0:hoff, :] = zeros
            hpad_ref[hoff:hoff + GRID_W, :] = a_ref[pl.ds(base, GRID_W), :].astype(F32)
            hpad_ref[hoff + GRID_W:hrow, :] = zeros
            xrow = hpad_ref[...]
            phases = [xrow] + [pltpu.roll(xrow, hrow - p, axis=0) for p in range(1, SUBLANES)]
            acc = bias
            for k in range(CONV_K):
                start = hoff - CONV_HALF + k
                ph, al = start % SUBLANES, start - start % SUBLANES
                acc = acc + taps[k] * phases[ph][al:al + GRID_W, :]
            y_ref[pl.ds(base, GRID_W), :] = acc
            return carry

        lax.fori_loop(0, rows, body, 0)


def _conv_call(a2, conv_w, conv_b, bsz, seq):
    m, c = a2.shape
    rows = seq // GRID_W
    groups = c // LANES
    kern = functools.partial(_conv_kernel, rows=rows, half_groups=groups // 2)
    return pl.pallas_call(
        kern,
        grid=(bsz, groups),
        in_specs=[pl.BlockSpec((seq, LANES), lambda b, j: (b, j)),
                  pl.BlockSpec((CONV_K, LANES), lambda b, j: (0, j)),
                  pl.BlockSpec((1, LANES), lambda b, j: (0, j))],
        out_specs=pl.BlockSpec((seq, LANES), lambda b, j: (b, j)),
        out_shape=jax.ShapeDtypeStruct((m, c), F32),
        scratch_shapes=[pltpu.VMEM((seq + 2 * CONV_HALF * GRID_W, LANES), F32),
                        pltpu.VMEM((GRID_W + 4 * SUBLANES, LANES), F32)],
        compiler_params=_params("parallel", "parallel"),
        name="axial_conv",
    )(a2, conv_w, conv_b)


def _gla_kernel(qf_ref, kf_ref, qb_ref, kb_ref, v_ref, cv_ref, sf0_ref, sb0_ref, of_ref, ob_ref, sf_ref, sb_ref,
                *, hk, hv, n_chunks):
    sf_ref[...] = sf0_ref[...]
    sb_ref[...] = sb0_ref[...]
    lower_mask = _tri(CHUNK, upper=False)
    upper_mask = _tri(CHUNK, upper=True)

    def body(i, carry):
        streams = []
        for c, q_ref, k_ref, o_ref, s_ref, mask, vec0 in ((i, qf_ref, kf_ref, of_ref, sf_ref, lower_mask, 0),
                                                          (n_chunks - 1 - i, qb_ref, kb_ref, ob_ref, sb_ref,
                                                           upper_mask, 3)):
            rows = pl.ds(pl.multiple_of(c * CHUNK, CHUNK), CHUNK)
            cv = cv_ref[pl.ds(pl.multiple_of(c * SUBLANES, SUBLANES), SUBLANES), :]
            q = q_ref[rows, :]
            k = k_ref[rows, :]
            for h in range(GLA_HEADS):
                ks = slice(h * hk, (h + 1) * hk)
                streams.append(dict(
                    q=q[:, ks], k=k[:, ks], v=v_ref[rows, h * hv:(h + 1) * hv], mask=mask, s_ref=s_ref, h=h,
                    o_ref=o_ref, rows=rows, e_mid=cv[vec0:vec0 + 1, ks], e_out=cv[vec0 + 1:vec0 + 2, ks],
                    d_end=cv[vec0 + 2:vec0 + 3, ks]))
        for st in streams:
            st["att"] = _dot_nt(st["q"], st["k"])
        for st in streams:
            st["s"] = st["s_ref"][st["h"]]
            st["s_in"] = (st["s"] * st["e_mid"]).astype(BF16)
        for st in streams:
            st["upd"] = _dot_tn(st["v"], st["k"])
        for st in streams:
            att = jnp.where(st["mask"], st["att"], 0.0).astype(BF16)
            o = _dot_nt(st["q"], st["s_in"]) + _dot(att, st["v"])
            st["o_ref"][st["rows"], st["h"] * hv:(st["h"] + 1) * hv] = o.astype(BF16)
        for st in streams:
            st["s_ref"][st["h"]] = st["s"] * st["d_end"] + st["upd"] * st["e_out"]
        return carry

    lax.fori_loop(0, n_chunks, body, 0, unroll=2)


def _gla_call(qf, kf, qb, kb, v2, cv, sf0, sb0, bsz, seq):
    dk, dv = qf.shape[1], v2.shape[1]
    hk, hv = dk // GLA_HEADS, dv // GLA_HEADS
    n_chunks = seq // CHUNK
    kern = functools.partial(_gla_kernel, hk=hk, hv=hv, n_chunks=n_chunks)
    tok = lambda width: pl.BlockSpec((seq, width), lambda b: (b, 0))
    state = pl.BlockSpec((None, GLA_HEADS, hv, hk), lambda b: (b, 0, 0, 0))
    o_sds = jax.ShapeDtypeStruct((bsz * seq, dv), BF16)
    return pl.pallas_call(
        kern,
        grid=(bsz,),
        in_specs=[tok(dk), tok(dk), tok(dk), tok(dk), tok(dv),
                  pl.BlockSpec((n_chunks * SUBLANES, dk), lambda b: (b, 0)), state, state],
        out_specs=[tok(dv), tok(dv)],
        out_shape=[o_sds, o_sds],
        scratch_shapes=[pltpu.VMEM((GLA_HEADS, hv, hk), F32), pltpu.VMEM((GLA_HEADS, hv, hk), F32)],
        compiler_params=_params("parallel"),
        name="gla_scan",
    )(qf, kf, qb, kb, v2, cv, sf0, sb0)


def _out_kernel(x_ref, gate_ref, y_ref, zs_ref, sgc_ref, of_ref, ob_ref, rs_ref, sgg_ref, lng_ref, lnb_ref, cproj_ref,
                hng_ref, gproj_ref, wout_ref, fng_ref, out_ref, *, hv):
    y = y_ref[...]
    mu = jnp.mean(y, axis=-1, keepdims=True)
    yc = y - mu
    var = jnp.mean(yc * yc, axis=-1, keepdims=True)
    act = _silu(yc * lax.rsqrt(var + EPS) * lng_ref[...] + lnb_ref[...]) * zs_ref[...].astype(F32)
    y_conv = _dot(act.astype(BF16), cproj_ref[...])

    o = of_ref[...].astype(F32) + ob_ref[...].astype(F32)
    normed = []
    for h in range(GLA_HEADS):
        oh = o[:, h * hv:(h + 1) * hv]
        ms = jnp.mean(oh * oh, axis=-1, keepdims=True)
        normed.append(oh * lax.rsqrt(ms + EPS) * hng_ref[...])
    og = jnp.concatenate(normed, axis=-1) * rs_ref[...].astype(F32)
    y_gla = _dot(og.astype(BF16), gproj_ref[...])

    merged = sgc_ref[...].astype(F32) * y_conv + sgg_ref[...].astype(F32) * y_gla
    h_new = x_ref[...] + gate_ref[...] * _dot(merged.astype(BF16), wout_ref[...])
    ms = jnp.mean(h_new * h_new, axis=-1, keepdims=True)
    out_ref[...] = h_new * lax.rsqrt(ms + EPS) * fng_ref[...]


def _out_call(x2, gate, y2, zs2, sgc2, of2, ob2, rs2, sgg2, lng, lnb, cproj, hng, gproj, wout, fng, seq, tm):
    m, d = x2.shape
    dv = of2.shape[1]
    per_b = seq // tm
    tok = lambda width: pl.BlockSpec((tm, width), lambda i: (i, 0))
    const = lambda shape: pl.BlockSpec(shape, lambda i: (0, 0))
    kern = functools.partial(_out_kernel, hv=dv // GLA_HEADS)
    return pl.pallas_call(
        kern,
        grid=(m // tm,),
        in_specs=[tok(d), pl.BlockSpec((None, 1, d), lambda i: (i // per_b, 0, 0)),
                  tok(d), tok(d), tok(d), tok(dv), tok(dv), tok(dv), tok(d),
                  const((1, d)), const((1, d)), const((d, d)),
                  const((1, dv // GLA_HEADS)), const((dv, d)), const((d, d)), const((1, d))],
        out_specs=tok(d),
        out_shape=jax.ShapeDtypeStruct((m, d), F32),
        compiler_params=_params("parallel"),
        name="out_stage",
    )(x2, gate, y2, zs2, sgc2, of2, ob2, rs2, sgg2, lng, lnb, cproj, hng, gproj, wout, fng)


def _layer_params(w_in, b_in, up_f, bias_f, up_b, bias_b, d, dk, dv):
    offs = {"d": d, "dk": dk, "dv": dv, "gv": 0, "gg": d, "z": 2 * d, "q": 3 * d, "k": 3 * d + dk, "v": 3 * d + 2 * dk,
            "r": 0, "mgc": dv, "mgg": dv + d}
    n_a = 3 * d + 2 * dk + dv
    n_d = 2 * GATE_RANK
    w_a, b_a = w_in[:, :n_a].astype(BF16), b_in[None, :n_a]
    w_t, b_t = w_in[:, n_a + n_d:].astype(BF16), b_in[None, n_a + n_d:]
    w_d = jnp.pad(w_in[:, n_a:n_a + n_d], ((0, 0), (0, LANES - n_d))).astype(BF16)
    b_d = jnp.pad(b_in[n_a:n_a + n_d], (0, LANES - n_d))[None, :]
    up = jnp.zeros((LANES, 2 * dk), F32)
    up = up.at[:GATE_RANK, :dk].set(up_f).at[GATE_RANK:n_d, dk:].set(up_b).astype(BF16)
    dbias = jnp.concatenate([bias_f, bias_b])[None, :]
    return w_a, b_a, w_t, b_t, w_d, b_d, up, dbias, offs


def kernel(x, c, ctx, c_ctx, ada_w, ada_b, norm_g, w_in, b_in, conv_w, conv_b, conv_ln_g, conv_ln_b, conv_proj,
           decay_up_fwd, decay_bias_fwd, decay_up_bwd, decay_bias_bwd, gla_norm_g, gla_proj, w_out, final_norm_g):
    bsz, seq, d = x.shape
    depth = ada_w.shape[0]
    assert depth == 1, "single-layer block"
    dk = decay_up_fwd.shape[-1]
    dv = gla_proj.shape[1]
    l = 0

    mod_rows = -(-(bsz + 1) // SUBLANES) * SUBLANES
    cs = jnp.concatenate([c, c_ctx[None, :], jnp.zeros((mod_rows - bsz - 1, d), F32)], axis=0)
    mod = _ada_call(cs, ada_w[l], ada_b[l][None, :])
    shift, scale, gate = (mod[:bsz, i * d:(i + 1) * d].reshape(bsz, 1, d) for i in range(3))
    shift_c, scale_c = (mod[bsz:bsz + 1, i * d:(i + 1) * d] for i in range(2))

    w_a, b_a, w_t, b_t, w_d, b_d, up, dbias, offs = _layer_params(
        w_in[l], b_in[l], decay_up_fwd[l], decay_bias_fwd[l], decay_up_bwd[l], decay_bias_bwd[l], d, dk, dv)
    ng = norm_g[l][None, :]

    sf0, sb0 = _ctx_call(ctx, shift_c, scale_c, ng, w_a, b_a, w_d, b_d, up, dbias, offs)

    x2 = x.reshape(bsz * seq, d)
    a2, zs2, v2, rs2, sgc2, sgg2, qf, kf, qb, kb, cv = _inproj_call(
        x2, shift, scale, ng, w_a, b_a, w_t, b_t, w_d, b_d, up, dbias, offs, seq, tm=512)
    y2 = _conv_call(a2, conv_w[l], conv_b[l][None, :], bsz, seq)
    of2, ob2 = _gla_call(qf, kf, qb, kb, v2, cv, sf0, sb0, bsz, seq)
    out = _out_call(x2, gate, y2, zs2, sgc2, of2, ob2, rs2, sgg2,
                    conv_ln_g[l][None, :], conv_ln_b[l][None, :], conv_proj[l].astype(BF16),
                    gla_norm_g[l][None, :], gla_proj[l].astype(BF16), w_out[l].astype(BF16),
                    final_norm_g[None, :], seq, tm=256)
    return out.reshape(bsz, seq, d)
```
